```python
import math
import jax, jax.numpy as jnp
from jax import lax
import numpy as np

D_MODEL = 1024
BATCH = 8
SEQ = 8192
DEPTH = 2

EPS = 1e-6
N_EVEN = (DEPTH + 1) // 2
N_ODD = DEPTH // 2
A_WIDTH = D_MODEL // 2
A_GROUPS = 8
CONV_A_WIDTH = 31
B_WIDTH = D_MODEL // 2
B_HEADS = 4
B_HEAD_DIM = B_WIDTH // B_HEADS
CHUNK = 128
EVEN_IN = 2 * A_WIDTH + 2 * B_WIDTH
EVEN_MIX = A_WIDTH + B_WIDTH
C_WIDTH = D_MODEL
CONV_C_WIDTH = 3
ODD_IN = 3 * C_WIDTH
PEER_HEADS = 8
N_KEYS = 128
N_EXPERTS = N_KEYS * N_KEYS
D_KEY = 256
D_KEY_HALF = D_KEY // 2
PEER_TOPK = 16
PEER_CHUNK = 128

kernel_name = "hybrid_conv_gmlp_shortconv_peer_adaln"


def rms_norm(x, g):
    xf = x.astype(jnp.float32)
    y = xf * lax.rsqrt(jnp.mean(xf * xf, axis=-1, keepdims=True) + EPS)
    return (y * g.astype(jnp.float32)).astype(x.dtype)


def layer_norm(x, g, b):
    xf = x.astype(jnp.float32)
    mu = jnp.mean(xf, axis=-1, keepdims=True)
    var = jnp.mean(jnp.square(xf - mu), axis=-1, keepdims=True)
    y = (xf - mu) * lax.rsqrt(var + EPS)
    return (y * g.astype(jnp.float32) + b.astype(jnp.float32)).astype(x.dtype)


def causal_dwconv(x, w):
    k, ch = w.shape
    return lax.conv_general_dilated(
        x, w.astype(x.dtype)[:, None, :], window_strides=(1,), padding=[(k - 1, 0)],
        dimension_numbers=("NWC", "WIO", "NWC"), feature_group_count=ch)


def modulate(h, shift, scale):
    return h * (1.0 + scale[:, None, :]) + shift[:, None, :]


def conv_gmlp_mixer(h, w_in, conv_w, conv_b, cln_g, cln_b, gln_g, gln_b, ws, bs, w_out):
    bsz, seq, _ = h.shape
    z = h @ w_in.astype(h.dtype)
    za, zb = z[..., :2 * A_WIDTH], z[..., 2 * A_WIDTH:]
    a = za[..., :A_WIDTH] * jax.nn.sigmoid(za[..., A_WIDTH:])
    a = causal_dwconv(a, conv_w) + conv_b.astype(h.dtype)
    a = jax.nn.silu(layer_norm(a, cln_g, cln_b))
    zb = jax.nn.gelu(zb, approximate=False)
    u, v = zb[..., :B_WIDTH], zb[..., B_WIDTH:]
    v = layer_norm(v, gln_g, gln_b)
    v = v.reshape(bsz, seq // CHUNK, CHUNK, B_HEADS, B_HEAD_DIM)
    causal = jnp.tril(jnp.ones((CHUNK, CHUNK), dtype=bool))
    ws_m = jnp.where(causal[None], ws, 0.0).astype(h.dtype)
    mixed = jnp.einsum("hts,bnshc->bnthc", ws_m, v) + bs.T.astype(h.dtype)[None, None, :, :, None]
    bout = u * mixed.reshape(bsz, seq, B_WIDTH)
    return jnp.concatenate([a, bout], axis=-1) @ w_out.astype(h.dtype)


def short_conv_mixer(h, w_in, conv_w, w_out):
    z = h @ w_in.astype(h.dtype)
    gb, gc, xin = z[..., :C_WIDTH], z[..., C_WIDTH:2 * C_WIDTH], z[..., 2 * C_WIDTH:]
    y = gb * causal_dwconv(gc * xin, conv_w)
    return y @ w_out.astype(h.dtype)


def peer(h, wq, subkeys, u_tab, v_tab):
    bsz, seq, d = h.shape
    sk = subkeys.astype(jnp.float32)

    def chunk_fn(hc):
        t = hc.shape[0]
        q = (hc @ wq.astype(hc.dtype)).reshape(t, PEER_HEADS, 2, D_KEY_HALF).astype(jnp.float32)
        s = jnp.einsum("thpk,pnk->thpn", q, sk)
        s1, i1 = lax.top_k(s[:, :, 0], PEER_TOPK)
        s2, i2 = lax.top_k(s[:, :, 1], PEER_TOPK)
        cand = (s1[..., :, None] + s2[..., None, :]).reshape(t, PEER_HEADS, PEER_TOPK * PEER_TOPK)
        best, j = lax.top_k(cand, PEER_TOPK)
        e = (jnp.take_along_axis(i1, j // PEER_TOPK, axis=-1) * N_KEYS
             + jnp.take_along_axis(i2, j % PEER_TOPK, axis=-1))
        gate = jax.nn.softmax(best, axis=-1)
        ue = u_tab[e].astype(hc.dtype)
        act = jax.nn.gelu(jnp.einsum("thkd,td->thk", ue, hc).astype(jnp.float32), approximate=False)
        ve = v_tab[e].astype(hc.dtype)
        return jnp.einsum("thk,thkd->td", (gate * act).astype(hc.dtype), ve)

    out = lax.map(chunk_fn, h.reshape(-1, PEER_CHUNK, d))
    return out.reshape(bsz, seq, d)


def setup_inputs(seed: int = 0) -> dict:
    key = jax.random.key(seed)
    ks = jax.random.split(key, 32)
    f32 = jnp.float32
    nrm = lambda k, shape, s: jax.random.normal(k, shape, f32) * s
    D = D_MODEL
    return {
        "x": nrm(ks[0], (BATCH, SEQ, D), 1.0),
        "c": nrm(ks[1], (BATCH, D), 1.0),
        "mod_w": nrm(ks[2], (DEPTH, D, 6 * D), 0.5 * D ** -0.5),
        "mod_b": nrm(ks[3], (DEPTH, 6 * D), 0.02),
        "norm1_g": 1.0 + nrm(ks[4], (DEPTH, D), 0.02),
        "norm2_g": 1.0 + nrm(ks[5], (DEPTH, D), 0.02),
        "ev_w_in": nrm(ks[6], (N_EVEN, D, EVEN_IN), D ** -0.5),
        "ev_conv_w": nrm(ks[7], (N_EVEN, CONV_A_WIDTH, A_WIDTH), CONV_A_WIDTH ** -0.5),
        "ev_conv_b": nrm(ks[8], (N_EVEN, A_WIDTH), 0.02),
        "ev_cln_g": 1.0 + nrm(ks[9], (N_EVEN, A_WIDTH), 0.02),
        "ev_cln_b": nrm(ks[10], (N_EVEN, A_WIDTH), 0.02),
        "ev_gln_g": 1.0 + nrm(ks[11], (N_EVEN, B_WIDTH), 0.02),
        "ev_gln_b": nrm(ks[12], (N_EVEN, B_WIDTH), 0.02),
        "ev_ws": nrm(ks[13], (N_EVEN, B_HEADS, CHUNK, CHUNK), CHUNK ** -0.5),
        "ev_bs": nrm(ks[14], (N_EVEN, B_HEADS, CHUNK), 0.02),
        "ev_w_out": nrm(ks[15], (N_EVEN, EVEN_MIX, D), EVEN_MIX ** -0.5),
        "od_w_in": nrm(ks[16], (N_ODD, D, ODD_IN), D ** -0.5),
        "od_conv_w": nrm(ks[17], (N_ODD, CONV_C_WIDTH, C_WIDTH), CONV_C_WIDTH ** -0.5),
        "od_w_out": nrm(ks[18], (N_ODD, C_WIDTH, D), C_WIDTH ** -0.5),
        "pk_wq": nrm(ks[19], (DEPTH, D, PEER_HEADS * D_KEY), D ** -0.5),
        "pk_subkeys": nrm(ks[20], (DEPTH, 2, N_KEYS, D_KEY_HALF), D_KEY_HALF ** -0.5),
        "pk_u": nrm(ks[21], (DEPTH, N_EXPERTS, D), D ** -0.5),
        "pk_v": nrm(ks[22], (DEPTH, N_EXPERTS, D), 1.0),
        "final_g": 1.0 + nrm(ks[23], (D,), 0.02),
    }


def reference(x, c, mod_w, mod_b, norm1_g, norm2_g, ev_w_in, ev_conv_w, ev_conv_b, ev_cln_g,
              ev_cln_b, ev_gln_g, ev_gln_b, ev_ws, ev_bs, ev_w_out, od_w_in, od_conv_w, od_w_out,
              pk_wq, pk_subkeys, pk_u, pk_v, final_g):
    c_act = jax.nn.silu(c)
    for l in range(DEPTH):
        mod = c_act @ mod_w[l].astype(c.dtype) + mod_b[l].astype(c.dtype)
        sh1, sc1, g1, sh2, sc2, g2 = jnp.split(mod.astype(x.dtype), 6, axis=-1)
        h = modulate(rms_norm(x, norm1_g[l]), sh1, sc1)
        i = l // 2
        if l % 2 == 0:
            y = conv_gmlp_mixer(h, ev_w_in[i], ev_conv_w[i], ev_conv_b[i], ev_cln_g[i], ev_cln_b[i],
                                ev_gln_g[i], ev_gln_b[i], ev_ws[i], ev_bs[i], ev_w_out[i])
        else:
            y = short_conv_mixer(h, od_w_in[i], od_conv_w[i], od_w_out[i])
        x = x + g1[:, None, :] * y
        h = modulate(rms_norm(x, norm2_g[l]), sh2, sc2)
        x = x + g2[:, None, :] * peer(h, pk_wq[l], pk_subkeys[l], pk_u[l], pk_v[l])
    return rms_norm(x, final_g)
```

```python
import functools

import jax
import jax.numpy as jnp
from jax import lax
from jax.experimental import pallas as pl
from jax.experimental.pallas import tpu as pltpu

F32 = jnp.float32
BF16 = jnp.bfloat16

EPS = 1e-6
A_WIDTH = 512
B_WIDTH = 512
B_HEADS = 4
CHUNK = 128
CONV_A_WIDTH = 31
CONV_C_WIDTH = 3
PEER_HEADS = 8
N_KEYS = 128
D_KEY_HALF = 128
PEER_TOPK = 16

LANES = 128
SUBLANES = 8
VMEM_LIMIT_BYTES = 56 * 1024 * 1024

SEQ_TILE = 512
TOKEN_BLOCK = 512
EXPERT_BLOCK = 1024
CONV_A_HALO = 32
CONV_C_HALO = 8
CONV_ROWS = 32

_NT = (((1,), (1,)), ((), ()))


def _rms_mod(x, g, shift, scale):
    y = x * lax.rsqrt(jnp.mean(x * x, axis=-1, keepdims=True) + EPS)
    return (y * g) * (1.0 + scale) + shift


def _layer_norm(x, g, b):
    mu = jnp.mean(x, axis=-1, keepdims=True)
    xc = x - mu
    var = jnp.mean(xc * xc, axis=-1, keepdims=True)
    return xc * lax.rsqrt(var + EPS) * g + b


def _gelu(x):
    return 0.5 * x * (1.0 + lax.erf(x * (2.0 ** -0.5)))


def _silu(x):
    return x * jax.nn.sigmoid(x)


def _mod_kernel(c_ref, w_ref, b_ref, o_ref):
    c = c_ref[...]
    o_ref[0] = jnp.dot(_silu(c), w_ref[0], preferred_element_type=F32,
                       precision=lax.Precision.HIGHEST) + b_ref[0]


def _mod_call(c, mod_w, mod_b):
    depth, d, six_d = mod_w.shape
    bsz = c.shape[0]
    nb = six_d // 4
    return pl.pallas_call(
        _mod_kernel,
        grid=(depth, six_d // nb),
        in_specs=[
            pl.BlockSpec((bsz, d), lambda l, j: (0, 0)),
            pl.BlockSpec((1, d, nb), lambda l, j: (l, 0, j)),
            pl.BlockSpec((1, 1, nb), lambda l, j: (l, 0, j)),
        ],
        out_specs=pl.BlockSpec((1, bsz, nb), lambda l, j: (l, 0, j)),
        out_shape=jax.ShapeDtypeStruct((depth, bsz, six_d), F32),
        compiler_params=pltpu.CompilerParams(
            dimension_semantics=("arbitrary", "arbitrary"),
            vmem_limit_bytes=VMEM_LIMIT_BYTES),
        name="adaln_mod",
    )(c, mod_w, mod_b.reshape(depth, 1, six_d))


def _even_kernel(x_ref, mod_ref, ng_ref, win_ref, cw_ref, cb_ref, clg_ref, clb_ref,
                 glg_ref, glb_ref, ws_ref, bs_ref, wout_ref, o_ref,
                 z_ref, ext_ref, cat_ref, *, ts):
    si = pl.program_id(1)
    x = x_ref[0]
    h = _rms_mod(x, ng_ref[...], mod_ref[0, 0:1, :], mod_ref[0, 1:2, :])
    z_ref[...] = jnp.dot(h.astype(BF16), win_ref[...], preferred_element_type=F32)

    @pl.when(si == 0)
    def _():
        ext_ref[0:CONV_A_HALO, :] = jnp.zeros((CONV_A_HALO, A_WIDTH), F32)

    ext_ref[CONV_A_HALO:CONV_A_HALO + ts, :] = (
        z_ref[:, 0:A_WIDTH] * jax.nn.sigmoid(z_ref[:, A_WIDTH:2 * A_WIDTH]))

    first_tap = CONV_A_HALO - (CONV_A_WIDTH - 1)

    for rc in range(ts // CONV_ROWS):
        r0 = rc * CONV_ROWS
        acc = jnp.broadcast_to(cb_ref[...], (CONV_ROWS, A_WIDTH))
        for k in range(CONV_A_WIDTH):
            t0 = r0 + first_tap + k
            acc = acc + ext_ref[t0:t0 + CONV_ROWS, :] * cw_ref[k:k + 1, :]
        a = _silu(_layer_norm(acc, clg_ref[...], clb_ref[...]))
        cat_ref[r0:r0 + CONV_ROWS, 0:A_WIDTH] = a.astype(BF16)
    ext_ref[0:CONV_A_HALO, :] = ext_ref[ts:ts + CONV_A_HALO, :]

    row = lax.broadcasted_iota(jnp.int32, (CHUNK, CHUNK), 0)
    col = lax.broadcasted_iota(jnp.int32, (CHUNK, CHUNK), 1)
    head_dim = B_WIDTH // B_HEADS

    def chunk_body(ci, carry):
        r0 = pl.multiple_of(ci * CHUNK, CHUNK)
        zb = _gelu(z_ref[pl.ds(r0, CHUNK), 2 * A_WIDTH:2 * A_WIDTH + 2 * B_WIDTH])
        u = zb[:, 0:B_WIDTH]
        v = _layer_norm(zb[:, B_WIDTH:], glg_ref[...], glb_ref[...]).astype(BF16)
        for hh in range(B_HEADS):
            cs = slice(hh * head_dim, (hh + 1) * head_dim)
            ws_m = jnp.where(row >= col, ws_ref[hh], 0.0).astype(BF16)
            mixed = jnp.dot(ws_m, v[:, cs], preferred_element_type=F32) + bs_ref[hh]
            cat_ref[pl.ds(r0, CHUNK), A_WIDTH + hh * head_dim:A_WIDTH + (hh + 1) * head_dim] = (
                (u[:, cs] * mixed).astype(BF16))
        return carry

    lax.fori_loop(0, ts // CHUNK, chunk_body, 0)

    y = jnp.dot(cat_ref[...], wout_ref[...], preferred_element_type=F32)
    o_ref[0] = x_ref[0] + mod_ref[0, 2:3, :] * y


def _even_call(x, mod, ng, w_in, conv_w, conv_b, cln_g, cln_b, gln_g, gln_b, ws, bs, w_out):
    bsz, seq, d = x.shape
    ts = SEQ_TILE
    even_in = w_in.shape[1]
    const2 = lambda b, s: (0, 0)
    const3 = lambda b, s: (0, 0, 0)
    bs_b = jnp.broadcast_to(bs[:, :, None], (B_HEADS, CHUNK, B_WIDTH // B_HEADS))
    return pl.pallas_call(
        functools.partial(_even_kernel, ts=ts),
        grid=(bsz, seq // ts),
        in_specs=[
            pl.BlockSpec((1, ts, d), lambda b, s: (b, s, 0)),
            pl.BlockSpec((1, 6, d), lambda b, s: (b, 0, 0)),
            pl.BlockSpec((1, d), const2),
            pl.BlockSpec((d, even_in), const2),
            pl.BlockSpec((CONV_A_WIDTH, A_WIDTH), const2),
            pl.BlockSpec((1, A_WIDTH), const2),
            pl.BlockSpec((1, A_WIDTH), const2),
            pl.BlockSpec((1, A_WIDTH), const2),
            pl.BlockSpec((1, B_WIDTH), const2),
            pl.BlockSpec((1, B_WIDTH), const2),
            pl.BlockSpec((B_HEADS, CHUNK, CHUNK), const3),
            pl.BlockSpec((B_HEADS, CHUNK, B_WIDTH // B_HEADS), const3),
            pl.BlockSpec((A_WIDTH + B_WIDTH, d), const2),
        ],
        out_specs=pl.BlockSpec((1, ts, d), lambda b, s: (b, s, 0)),
        out_shape=jax.ShapeDtypeStruct(x.shape, F32),
        scratch_shapes=[
            pltpu.VMEM((ts, even_in), F32),
            pltpu.VMEM((ts + CONV_A_HALO, A_WIDTH), F32),
            pltpu.VMEM((ts, A_WIDTH + B_WIDTH), BF16),
        ],
        compiler_params=pltpu.CompilerParams(
            dimension_semantics=("arbitrary", "arbitrary"),
            vmem_limit_bytes=VMEM_LIMIT_BYTES),
        name="even_mixer",
    )(x, mod, ng.reshape(1, d), w_in.astype(BF16), conv_w, conv_b.reshape(1, -1),
      cln_g.reshape(1, -1), cln_b.reshape(1, -1), gln_g.reshape(1, -1), gln_b.reshape(1, -1),
      ws, bs_b, w_out.astype(BF16))


def _odd_kernel(x_ref, mod_ref, ng_ref, win_ref, cw_ref, wout_ref, o_ref,
                z_ref, ext_ref, cat_ref, *, ts):
    si = pl.program_id(1)
    cwd = cat_ref.shape[1]
    x = x_ref[0]
    h = _rms_mod(x, ng_ref[...], mod_ref[0, 0:1, :], mod_ref[0, 1:2, :])
    z_ref[...] = jnp.dot(h.astype(BF16), win_ref[...], preferred_element_type=F32)

    @pl.when(si == 0)
    def _():
        ext_ref[0:CONV_C_HALO, :] = jnp.zeros((CONV_C_HALO, cwd), F32)

    ext_ref[CONV_C_HALO:CONV_C_HALO + ts, :] = z_ref[:, cwd:2 * cwd] * z_ref[:, 2 * cwd:3 * cwd]
    first_tap = CONV_C_HALO - (CONV_C_WIDTH - 1)

    for rc in range(ts // CONV_ROWS):
        r0 = rc * CONV_ROWS
        t0 = r0 + first_tap
        acc = ext_ref[t0:t0 + CONV_ROWS, :] * cw_ref[0:1, :]
        for k in range(1, CONV_C_WIDTH):
            acc = acc + ext_ref[t0 + k:t0 + k + CONV_ROWS, :] * cw_ref[k:k + 1, :]
        cat_ref[r0:r0 + CONV_ROWS, :] = (z_ref[r0:r0 + CONV_ROWS, 0:cwd] * acc).astype(BF16)
    ext_ref[0:CONV_C_HALO, :] = ext_ref[ts:ts + CONV_C_HALO, :]

    y = jnp.dot(cat_ref[...], wout_ref[...], preferred_element_type=F32)
    o_ref[0] = x_ref[0] + mod_ref[0, 2:3, :] * y


def _odd_call(x, mod, ng, w_in, conv_w, w_out):
    bsz, seq, d = x.shape
    ts = SEQ_TILE
    odd_in = w_in.shape[1]
    cwd = odd_in // 3
    const2 = lambda b, s: (0, 0)
    return pl.pallas_call(
        functools.partial(_odd_kernel, ts=ts),
        grid=(bsz, seq // ts),
        in_specs=[
            pl.BlockSpec((1, ts, d), lambda b, s: (b, s, 0)),
            pl.BlockSpec((1, 6, d), lambda b, s: (b, 0, 0)),
            pl.BlockSpec((1, d), const2),
            pl.BlockSpec((d, odd_in), const2),
            pl.BlockSpec((CONV_C_WIDTH, cwd), const2),
            pl.BlockSpec((cwd, d), const2),
        ],
        out_specs=pl.BlockSpec((1, ts, d), lambda b, s: (b, s, 0)),
        out_shape=jax.ShapeDtypeStruct(x.shape, F32),
        scratch_shapes=[
            pltpu.VMEM((ts, odd_in), F32),
            pltpu.VMEM((ts + CONV_C_HALO, cwd), F32),
            pltpu.VMEM((ts, cwd), BF16),
        ],
        compiler_params=pltpu.CompilerParams(
            dimension_semantics=("arbitrary", "arbitrary"),
            vmem_limit_bytes=VMEM_LIMIT_BYTES),
        name="odd_mixer",
    )(x, mod, ng.reshape(1, d), w_in.astype(BF16), conv_w, w_out.astype(BF16))


def _sorted_top(s, k):
    vals = []
    work = s
    for _ in range(k):
        m = jnp.max(work, axis=0, keepdims=True)
        vals.append(m)
        work = jnp.where(work == m, -jnp.inf, work)
    return jnp.concatenate(vals, axis=0)


def _kth_largest_pair_sum(a, b, k):
    pieces = [a[0:1, :] + b[0:SUBLANES, :], a[0:1, :] + b[SUBLANES:, :]]
    for r in range(1, SUBLANES):
        pieces.append(a[r:r + 1, :] + b[0:SUBLANES, :])
    pieces.append(a[SUBLANES:, :] + b[0:1, :])
    tau = None
    for _ in range(k):
        m = pieces[0]
        for p in pieces[1:]:
            m = jnp.maximum(m, p)
        tau = jnp.max(m, axis=0, keepdims=True)
        pieces = [jnp.where(p == tau, -jnp.inf, p) for p in pieces]
    return tau


def _peer_kernel(*refs, tb, ni, final):
    if final:
        (x_ref, mod_ref, ng_ref, wq_ref, sk_ref, u_ref, vt_ref, fg_ref, o_ref,
         h_ref, q_ref, s2_ref, e2_ref, thr_ref, w_ref, a_ref, ga_ref, acc_ref) = refs
    else:
        (x_ref, mod_ref, ng_ref, wq_ref, sk_ref, u_ref, vt_ref, o_ref,
         h_ref, q_ref, s2_ref, e2_ref, thr_ref, w_ref, a_ref, ga_ref, acc_ref) = refs
        fg_ref = None
    e = pl.program_id(2)
    n_e = pl.num_programs(2)

    @pl.when(e == 0)
    def _prologue():
        x = x_ref[0]
        h = _rms_mod(x, ng_ref[...], mod_ref[0, 3:4, :], mod_ref[0, 4:5, :])
        hb = h.astype(BF16)
        h_ref[...] = hb
        q = jnp.dot(hb, wq_ref[...], preferred_element_type=F32)
        for hp in range(2 * PEER_HEADS):
            q_ref[hp] = q[:, hp * D_KEY_HALF:(hp + 1) * D_KEY_HALF]
        acc_ref[...] = jnp.zeros_like(acc_ref)

        def head_body(hh, carry):
            s1 = lax.dot_general(sk_ref[0], q_ref[2 * hh], _NT, preferred_element_type=F32,
                                 precision=lax.Precision.HIGHEST)
            s2 = lax.dot_general(sk_ref[1], q_ref[2 * hh + 1], _NT, preferred_element_type=F32,
                                 precision=lax.Precision.HIGHEST)
            a = _sorted_top(s1, PEER_TOPK)
            b = _sorted_top(s2, PEER_TOPK)
            tau = _kth_largest_pair_sum(a, b, PEER_TOPK)
            ea = jnp.exp(a - a[0:1, :])
            eb = jnp.exp(b - b[0:1, :])
            thr_rows = []
            z = jnp.zeros_like(tau)
            for r in range(PEER_TOPK):
                sel = (a[r:r + 1, :] + b) >= tau
                thr_rows.append(jnp.min(jnp.where(sel, b, jnp.inf), axis=0, keepdims=True))
                z = z + ea[r:r + 1, :] * jnp.sum(jnp.where(sel, eb, 0.0), axis=0, keepdims=True)
            thr = jnp.full(s1.shape, jnp.inf, F32)
            for r in range(PEER_TOPK):
                thr = jnp.where(s1 == a[r:r + 1, :], thr_rows[r], thr)
            s2_ref[hh] = s2
            e2_ref[hh] = jnp.exp(s2 - b[0:1, :])
            thr_ref[hh] = thr
            w_ref[hh] = jnp.exp(s1 - a[0:1, :]) / z
            return carry

        lax.fori_loop(0, PEER_HEADS, head_body, 0)

    a_ref[...] = lax.dot_general(u_ref[...], h_ref[...], _NT, preferred_element_type=F32)

    def key_group_body(ig, carry):
        i0 = pl.multiple_of(e * ni + ig * SUBLANES, SUBLANES)
        for lg in range(tb // LANES):
            ls = slice(lg * LANES, (lg + 1) * LANES)
            thr8 = [thr_ref[hh, pl.ds(i0, SUBLANES), ls] for hh in range(PEER_HEADS)]
            w8 = [w_ref[hh, pl.ds(i0, SUBLANES), ls] for hh in range(PEER_HEADS)]
            for k in range(SUBLANES):
                r0 = pl.multiple_of((ig * SUBLANES + k) * N_KEYS, N_KEYS)
                g = jnp.zeros((N_KEYS, LANES), F32)
                for hh in range(PEER_HEADS):
                    sel = s2_ref[hh, :, ls] >= thr8[hh][k:k + 1, :]
                    g = g + jnp.where(sel, e2_ref[hh, :, ls] * w8[hh][k:k + 1, :], 0.0)
                act = _gelu(a_ref[pl.ds(r0, N_KEYS), ls])
                ga_ref[pl.ds(r0, N_KEYS), ls] = (g * act).astype(BF16)
        return carry

    lax.fori_loop(0, ni // SUBLANES, key_group_body, 0)

    acc_ref[...] += jnp.dot(vt_ref[...], ga_ref[...], preferred_element_type=F32)

    @pl.when(e == n_e - 1)
    def _epilogue():
        out = acc_ref[...].T
        xn = x_ref[0] + mod_ref[0, 5:6, :] * out
        if final:
            xn = xn * lax.rsqrt(jnp.mean(xn * xn, axis=-1, keepdims=True) + EPS) * fg_ref[...]
        o_ref[0] = xn


def _peer_call(x, mod, ng, wq, sk, u_bf, vt_bf, final_g):
    bsz, seq, d = x.shape
    n_exp = u_bf.shape[0]
    tb, eb = TOKEN_BLOCK, EXPERT_BLOCK
    ni = eb // N_KEYS
    assert ni % SUBLANES == 0 and tb % LANES == 0
    final = final_g is not None
    const2 = lambda b, s, e: (0, 0)
    in_specs = [
        pl.BlockSpec((1, tb, d), lambda b, s, e: (b, s, 0)),
        pl.BlockSpec((1, 6, d), lambda b, s, e: (b, 0, 0)),
        pl.BlockSpec((1, d), const2),
        pl.BlockSpec(wq.shape, const2),
        pl.BlockSpec(sk.shape, lambda b, s, e: (0, 0, 0)),
        pl.BlockSpec((eb, d), lambda b, s, e: (e, 0)),
        pl.BlockSpec((d, eb), lambda b, s, e: (0, e)),
    ]
    args = [x, mod, ng.reshape(1, d), wq, sk, u_bf, vt_bf]
    if final:
        in_specs.append(pl.BlockSpec((1, d), const2))
        args.append(final_g.reshape(1, d))
    return pl.pallas_call(
        functools.partial(_peer_kernel, tb=tb, ni=ni, final=final),
        grid=(bsz, seq // tb, n_exp // eb),
        in_specs=in_specs,
        out_specs=pl.BlockSpec((1, tb, d), lambda b, s, e: (b, s, 0)),
        out_shape=jax.ShapeDtypeStruct(x.shape, F32),
        scratch_shapes=[
            pltpu.VMEM((tb, d), BF16),
            pltpu.VMEM((2 * PEER_HEADS, tb, D_KEY_HALF), F32),
            pltpu.VMEM((PEER_HEADS, N_KEYS, tb), F32),
            pltpu.VMEM((PEER_HEADS, N_KEYS, tb), F32),
            pltpu.VMEM((PEER_HEADS, N_KEYS, tb), F32),
            pltpu.VMEM((PEER_HEADS, N_KEYS, tb), F32),
            pltpu.VMEM((eb, tb), F32),
            pltpu.VMEM((eb, tb), BF16),
            pltpu.VMEM((d, tb), F32),
        ],
        compiler_params=pltpu.CompilerParams(
            dimension_semantics=("arbitrary", "arbitrary", "arbitrary"),
            vmem_limit_bytes=VMEM_LIMIT_BYTES),
        name="peer_final" if final else "peer",
    )(*args)


def kernel(x, c, mod_w, mod_b, norm1_g, norm2_g, ev_w_in, ev_conv_w, ev_conv_b, ev_cln_g, ev_cln_b, ev_gln_g, ev_gln_b, ev_ws, ev_bs, ev_w_out, od_w_in, od_conv_w, od_w_out, pk_wq, pk_subkeys, pk_u, pk_v, final_g):
    depth = mod_w.shape[0]
    bsz, seq, d = x.shape
    assert seq % SEQ_TILE == 0 and seq % TOKEN_BLOCK == 0 and pk_u.shape[1] % EXPERT_BLOCK == 0
    mod = _mod_call(c, mod_w, mod_b).reshape(depth, bsz, 6, d)
    for l in range(depth):
        i = l // 2
        if l % 2 == 0:
            x = _even_call(x, mod[l], norm1_g[l], ev_w_in[i], ev_conv_w[i], ev_conv_b[i],
                           ev_cln_g[i], ev_cln_b[i], ev_gln_g[i], ev_gln_b[i], ev_ws[i], ev_bs[i],
                           ev_w_out[i])
        else:
            x = _odd_call(x, mod[l], norm1_g[l], od_w_in[i], od_conv_w[i], od_w_out[i])
        x = _peer_call(x, mod[l], norm2_g[l], pk_wq[l].astype(BF16), pk_subkeys[l],
                       pk_u[l].astype(BF16), pk_v[l].T.astype(BF16),
                       final_g if l == depth - 1 else None)
    return x
```
